```python
import math
import jax, jax.numpy as jnp
from jax import lax
import numpy as np

D_MODEL = 1024
BATCH = 16
SEQ = 4096
DEPTH = 4
DEC_BATCH = 2
DEC_SEQ = 16384
PAST_LEN = 128

N_MIXERS = 3
D_FF = 2816
N_ADA = 9
EPS = 1e-6
SG_CHUNK = 128
SG_D_FFN = 6 * D_MODEL
SG_HALF = SG_D_FFN // 2
SG_GROUPS = 8
SG_GROUP_DIM = SG_HALF // SG_GROUPS
DA_HEADS = 8
DA_HEAD_DIM = D_MODEL // (2 * DA_HEADS)
DA_V_DIM = 2 * DA_HEAD_DIM
ROT_DIM = DA_HEAD_DIM // 4
ROPE_THETA = 500000.0
Q_BLOCK = 128
CONV_WIDTH = 3

N_LAYERS_A = (DEPTH + 2) // 3
N_LAYERS_B = (DEPTH + 1) // 3
N_LAYERS_C = DEPTH // 3

kernel_name = "hybrid_bidir_encoder_macaron_adaln"


def rms_norm(x, g):
    xf = x.astype(jnp.float32)
    y = xf * lax.rsqrt(jnp.mean(xf * xf, axis=-1, keepdims=True) + EPS)
    return (y * g.astype(jnp.float32)).astype(x.dtype)


def layer_norm(x, g, b):
    xf = x.astype(jnp.float32)
    mu = jnp.mean(xf, axis=-1, keepdims=True)
    xc = xf - mu
    y = xc * lax.rsqrt(jnp.mean(xc * xc, axis=-1, keepdims=True) + EPS)
    return (y * g.astype(jnp.float32) + b.astype(jnp.float32)).astype(x.dtype)


def modulate(h, shift, scale):
    return h * (1 + scale[:, None, :]) + shift[:, None, :]


def swiglu(h, w_gate, w_up, w_down):
    return (jax.nn.silu(h @ w_gate) * (h @ w_up)) @ w_down


def spatial_gating_mixer(h, w_in, b_in, ln_g, ln_b, w_s, b_s, w_out):
    bsz, s, _ = h.shape
    z = jax.nn.gelu(h @ w_in + b_in, approximate=False)
    u, v = jnp.split(z, 2, axis=-1)
    v = layer_norm(v, ln_g, ln_b)
    v = v.reshape(bsz, s // SG_CHUNK, SG_CHUNK, SG_GROUPS, SG_GROUP_DIM)
    vs = jnp.einsum('gpq,bnqgd->bnpgd', w_s, v) + b_s.T[:, :, None]
    return (u * vs.reshape(bsz, s, SG_HALF)) @ w_out


def partial_rotary(x, cos, sin):
    half = ROT_DIM // 2
    x1 = x[..., :half]
    x2 = x[..., half:ROT_DIM]
    rest = x[..., ROT_DIM:]
    c = cos[None, :, None, None, :]
    s = sin[None, :, None, None, :]
    return jnp.concatenate([x1 * c - x2 * s, x2 * c + x1 * s, rest], axis=-1)


def diff_attention_mixer(h, w_qkv, lam, subln_g, w_out, lambda_init):
    bsz, s, _ = h.shape
    q, k, v = jnp.split(h @ w_qkv, 3, axis=-1)
    q = q.reshape(bsz, s, DA_HEADS, 2, DA_HEAD_DIM)
    k = k.reshape(bsz, s, DA_HEADS, 2, DA_HEAD_DIM)
    v = v.reshape(bsz, s, DA_HEADS, DA_V_DIM)
    pos = jnp.arange(s, dtype=jnp.float32)
    inv_freq = ROPE_THETA ** (-jnp.arange(0, ROT_DIM, 2, dtype=jnp.float32) / ROT_DIM)
    ang = pos[:, None] * inv_freq[None, :]
    cos = jnp.cos(ang).astype(h.dtype)
    sin = jnp.sin(ang).astype(h.dtype)
    q = partial_rotary(q, cos, sin) * (DA_HEAD_DIM ** -0.5)
    k = partial_rotary(k, cos, sin)
    lamf = lam.astype(jnp.float32)
    lambda_full = (jnp.exp(jnp.sum(lamf[0] * lamf[1])) - jnp.exp(jnp.sum(lamf[2] * lamf[3]))
                   + lambda_init)
    q_blocks = q.reshape(bsz, s // Q_BLOCK, Q_BLOCK, DA_HEADS, 2, DA_HEAD_DIM).transpose(1, 0, 2, 3, 4, 5)

    def attend(qb):
        scores = jnp.einsum('bqhcd,bkhcd->bhcqk', qb, k).astype(jnp.float32)
        p = jax.nn.softmax(scores, axis=-1)
        a = (p[:, :, 0] - lambda_full * p[:, :, 1]).astype(v.dtype)
        return jnp.einsum('bhqk,bkhe->bqhe', a, v)

    o = lax.map(attend, q_blocks)
    o = o.transpose(1, 0, 2, 3, 4).reshape(bsz, s, DA_HEADS, DA_V_DIM)
    o = rms_norm(o, subln_g) * (1.0 - lambda_init)
    return o.reshape(bsz, s, D_MODEL) @ w_out


def short_conv_mixer(h, w_in, conv_w, w_out):
    b_gate, c_gate, hv = jnp.split(h @ w_in, 3, axis=-1)
    g = jnp.pad(c_gate * hv, ((0, 0), (1, 1), (0, 0)))
    conv = g[:, :-2] * conv_w[0] + g[:, 1:-1] * conv_w[1] + g[:, 2:] * conv_w[2]
    return (b_gate * conv) @ w_out


def run_trunk(x, c, norm_g, ada_w, ada_b, ffn_w_gate, ffn_w_up, ffn_w_down,
              sg_w_in, sg_b_in, sg_ln_g, sg_ln_b, sg_w_s, sg_b_s, sg_w_out,
              da_w_qkv, da_lambda, da_subln_g, da_w_out,
              conv_w_in, conv_kernel, conv_w_out, final_norm_g):
    c_act = jax.nn.silu(c)
    for i in range(DEPTH):
        mod = c_act @ ada_w[i] + ada_b[i]
        sh1, sc1, g1, sh2, sc2, g2, sh3, sc3, g3 = jnp.split(mod, N_ADA, axis=-1)
        h = modulate(rms_norm(x, norm_g[i, 0]), sh1, sc1)
        x = x + 0.5 * g1[:, None, :] * swiglu(h, ffn_w_gate[i, 0], ffn_w_up[i, 0], ffn_w_down[i, 0])
        h = modulate(rms_norm(x, norm_g[i, 1]), sh2, sc2)
        j = i // N_MIXERS
        kind = i % N_MIXERS
        if kind == 0:
            y = spatial_gating_mixer(h, sg_w_in[j], sg_b_in[j], sg_ln_g[j], sg_ln_b[j],
                                     sg_w_s[j], sg_b_s[j], sg_w_out[j])
        elif kind == 1:
            lambda_init = 0.8 - 0.6 * math.exp(-0.3 * i)
            y = diff_attention_mixer(h, da_w_qkv[j], da_lambda[j], da_subln_g[j], da_w_out[j], lambda_init)
        else:
            y = short_conv_mixer(h, conv_w_in[j], conv_kernel[j], conv_w_out[j])
        x = x + g2[:, None, :] * y
        h = modulate(rms_norm(x, norm_g[i, 2]), sh3, sc3)
        x = x + 0.5 * g3[:, None, :] * swiglu(h, ffn_w_gate[i, 1], ffn_w_up[i, 1], ffn_w_down[i, 1])
    return rms_norm(x, final_norm_g)


def setup_inputs(seed: int = 0) -> dict:
    key = jax.random.key(seed)
    ks = jax.random.split(key, 25)
    f32 = jnp.float32

    def nrm(k, shape, scale):
        return jax.random.normal(k, shape, f32) * scale

    D = D_MODEL
    return {
        "x_prompt": nrm(ks[0], (BATCH, SEQ, D), 1.0),
        "x_sample": nrm(ks[1], (DEC_BATCH, DEC_SEQ, D), 1.0),
        "c_prompt": nrm(ks[2], (BATCH, D), 1.0),
        "c_sample": nrm(ks[3], (DEC_BATCH, D), 1.0),
        "norm_g": 1.0 + nrm(ks[4], (DEPTH, 3, D), 0.1),
        "ada_w": nrm(ks[5], (DEPTH, D, N_ADA * D), 0.5 * D ** -0.5),
        "ada_b": nrm(ks[6], (DEPTH, N_ADA * D), 0.02),
        "ffn_w_gate": nrm(ks[7], (DEPTH, 2, D, D_FF), D ** -0.5),
        "ffn_w_up": nrm(ks[8], (DEPTH, 2, D, D_FF), D ** -0.5),
        "ffn_w_down": nrm(ks[9], (DEPTH, 2, D_FF, D), D_FF ** -0.5),
        "sg_w_in": nrm(ks[10], (N_LAYERS_A, D, SG_D_FFN), D ** -0.5),
        "sg_b_in": nrm(ks[11], (N_LAYERS_A, SG_D_FFN), 0.02),
        "sg_ln_g": 1.0 + nrm(ks[12], (N_LAYERS_A, SG_HALF), 0.1),
        "sg_ln_b": nrm(ks[13], (N_LAYERS_A, SG_HALF), 0.02),
        "sg_w_s": nrm(ks[14], (N_LAYERS_A, SG_GROUPS, SG_CHUNK, SG_CHUNK), SG_CHUNK ** -0.5),
        "sg_b_s": 1.0 + nrm(ks[15], (N_LAYERS_A, SG_GROUPS, SG_CHUNK), 0.1),
        "sg_w_out": nrm(ks[16], (N_LAYERS_A, SG_HALF, D), SG_HALF ** -0.5),
        "da_w_qkv": nrm(ks[17], (N_LAYERS_B, D, 3 * D), D ** -0.5),
        "da_lambda": nrm(ks[18], (N_LAYERS_B, 4, DA_HEAD_DIM), 0.1),
        "da_subln_g": 1.0 + nrm(ks[19], (N_LAYERS_B, DA_V_DIM), 0.1),
        "da_w_out": nrm(ks[20], (N_LAYERS_B, D, D), D ** -0.5),
        "conv_w_in": nrm(ks[21], (N_LAYERS_C, D, 3 * D), D ** -0.5),
        "conv_kernel": nrm(ks[22], (N_LAYERS_C, CONV_WIDTH, D), CONV_WIDTH ** -0.5),
        "conv_w_out": nrm(ks[23], (N_LAYERS_C, D, D), D ** -0.5),
        "final_norm_g": 1.0 + nrm(ks[24], (D,), 0.1),
    }


def reference(x_prompt, x_sample, c_prompt, c_sample, norm_g, ada_w, ada_b,
              ffn_w_gate, ffn_w_up, ffn_w_down,
              sg_w_in, sg_b_in, sg_ln_g, sg_ln_b, sg_w_s, sg_b_s, sg_w_out,
              da_w_qkv, da_lambda, da_subln_g, da_w_out,
              conv_w_in, conv_kernel, conv_w_out, final_norm_g):
    y_prompt = run_trunk(x_prompt, c_prompt, norm_g, ada_w, ada_b, ffn_w_gate, ffn_w_up, ffn_w_down,
                         sg_w_in, sg_b_in, sg_ln_g, sg_ln_b, sg_w_s, sg_b_s, sg_w_out,
                         da_w_qkv, da_lambda, da_subln_g, da_w_out,
                         conv_w_in, conv_kernel, conv_w_out, final_norm_g)
    y_sample = run_trunk(x_sample, c_sample, norm_g, ada_w, ada_b, ffn_w_gate, ffn_w_up, ffn_w_down,
                         sg_w_in, sg_b_in, sg_ln_g, sg_ln_b, sg_w_s, sg_b_s, sg_w_out,
                         da_w_qkv, da_lambda, da_subln_g, da_w_out,
                         conv_w_in, conv_kernel, conv_w_out, final_norm_g)
    return (y_prompt, y_sample)
```

```python
import functools
import math

import jax
import jax.numpy as jnp
from jax import lax
from jax.experimental import pallas as pl
from jax.experimental.pallas import tpu as pltpu

F32 = jnp.float32
BF16 = jnp.bfloat16
EPS = 1e-6

N_ADA = 9
N_MIXERS = 3
SG_CHUNK = 128
SG_GROUPS = 8
DA_HEADS = 8
ROPE_THETA = 500000.0
V7X_VMEM_LIMIT_BYTES = 56 * 1024 * 1024
LANES = 128
BF16_ROWS = 16

ROW_TILE = 512
FFN_COL_CHUNK = 256
Q_TILE = 256
KV_TILE = 512


def _cparams(n_axes):
    return pltpu.CompilerParams(
        dimension_semantics=("arbitrary",) * n_axes,
        vmem_limit_bytes=V7X_VMEM_LIMIT_BYTES,
    )


def _const_spec(shape):
    zeros = (0,) * len(shape)
    return pl.BlockSpec(shape, lambda *_: zeros, pipeline_mode=pl.Buffered(1))


def _rms(x, g):
    ms = jnp.mean(x * x, axis=-1, keepdims=True)
    return x * lax.rsqrt(ms + EPS) * g


def _norm_mod(x, g, mod_ref, row):
    shift = mod_ref[row:row + 1, :]
    scale = mod_ref[row + 1:row + 2, :]
    return _rms(x, g) * (1.0 + scale) + shift


def _silu(x):
    return x * jax.nn.sigmoid(x)


def _gelu(x):
    return 0.5 * x * (1.0 + lax.erf(x * math.sqrt(0.5)))


def _dot(a, b):
    return jnp.dot(a, b, preferred_element_type=F32)


def _ada_kernel(c_ref, w_ref, b_ref, o_ref):
    a = _silu(c_ref[...]).astype(BF16)
    o_ref[...] = _dot(a, w_ref[...].astype(BF16)) + b_ref[...]


def _ada_mod(c_all, ada_w, ada_b):
    depth, d, nd = ada_w.shape
    bt = c_all.shape[0]
    tn = d
    out = pl.pallas_call(
        _ada_kernel,
        grid=(depth, nd // tn),
        in_specs=[
            pl.BlockSpec((bt, d), lambda l, j: (0, 0)),
            pl.BlockSpec((None, d, tn), lambda l, j: (l, 0, j)),
            pl.BlockSpec((None, 1, tn), lambda l, j: (l, 0, j)),
        ],
        out_specs=pl.BlockSpec((None, bt, tn), lambda l, j: (l, 0, j)),
        out_shape=jax.ShapeDtypeStruct((depth, bt, nd), F32),
        compiler_params=_cparams(2),
        name="ada_mod",
    )(c_all, ada_w, ada_b.reshape(depth, 1, nd))
    return out.reshape(depth, bt, N_ADA, d)


def _mod_spec(layer, boff, tiles_per_seq, d):
    return pl.BlockSpec((None, None, N_ADA, d),
                        lambda i, *_: (layer, boff + i // tiles_per_seq, 0, 0))


def _ffn_kernel(x_ref, mod_ref, ng_ref, wg_ref, wu_ref, wd_ref, *rest, mod_row, final):
    if final:
        fg_ref, o_ref, act_ref = rest
    else:
        o_ref, act_ref = rest
    x = x_ref[...]
    h = _norm_mod(x, ng_ref[...], mod_ref, mod_row).astype(BF16)
    d_ff = wg_ref.shape[1]
    for c in range(d_ff // FFN_COL_CHUNK):
        sl = slice(c * FFN_COL_CHUNK, (c + 1) * FFN_COL_CHUNK)
        g = _dot(h, wg_ref[:, sl])
        u = _dot(h, wu_ref[:, sl])
        act_ref[:, sl] = (_silu(g) * u).astype(BF16)
    y = _dot(act_ref[...], wd_ref[...])
    gate = mod_ref[mod_row + 2:mod_row + 3, :]
    out = x + (0.5 * gate) * y
    if final:
        out = _rms(out, fg_ref[...])
    o_ref[...] = out


def _ffn(x, mod, layer, boff, seq, ng, wg, wu, wd, mod_row, final_g=None):
    t, d = x.shape
    d_ff = wg.shape[1]
    tm = min(ROW_TILE, seq)
    final = final_g is not None
    in_specs = [
        pl.BlockSpec((tm, d), lambda i: (i, 0)),
        _mod_spec(layer, boff, seq // tm, d),
        _const_spec((1, d)),
        _const_spec((d, d_ff)),
        _const_spec((d, d_ff)),
        _const_spec((d_ff, d)),
    ]
    args = [x, mod, ng, wg, wu, wd]
    if final:
        in_specs.append(_const_spec((1, d)))
        args.append(final_g)
    return pl.pallas_call(
        functools.partial(_ffn_kernel, mod_row=mod_row, final=final),
        grid=(t // tm,),
        in_specs=in_specs,
        out_specs=pl.BlockSpec((tm, d), lambda i: (i, 0)),
        out_shape=jax.ShapeDtypeStruct((t, d), F32),
        scratch_shapes=[pltpu.VMEM((tm, d_ff), BF16)],
        compiler_params=_cparams(1),
        name="ffn",
    )(*args)


def _sg_kernel(x_ref, mod_ref, ng_ref, win_ref, bin_ref, lng_ref, lnb_ref, ws_ref, bs_ref,
               wout_ref, o_ref, v_ref, vn_ref, gated_ref):
    x = x_ref[...]
    tm = x.shape[0]
    half = wout_ref.shape[0]
    gdim = half // SG_GROUPS
    h = _norm_mod(x, ng_ref[...], mod_ref, 3).astype(BF16)
    for g in range(SG_GROUPS):
        sl = slice(g * gdim, (g + 1) * gdim)
        sv = slice(half + g * gdim, half + (g + 1) * gdim)
        v_ref[:, sl] = _gelu(_dot(h, win_ref[:, sv]) + bin_ref[:, sv])
    v = v_ref[...]
    mu = jnp.mean(v, axis=-1, keepdims=True)
    vc = v - mu
    var = jnp.mean(vc * vc, axis=-1, keepdims=True)
    vn_ref[...] = (vc * lax.rsqrt(var + EPS) * lng_ref[...] + lnb_ref[...]).astype(BF16)
    for g in range(SG_GROUPS):
        sl = slice(g * gdim, (g + 1) * gdim)
        u = _gelu(_dot(h, win_ref[:, sl]) + bin_ref[:, sl])
        bias = jnp.concatenate([bs_ref[g]] * (gdim // LANES), axis=1)
        for n in range(tm // SG_CHUNK):
            rows = slice(n * SG_CHUNK, (n + 1) * SG_CHUNK)
            vs = _dot(ws_ref[g], vn_ref[rows, sl]) + bias
            gated_ref[rows, sl] = (u[rows, :] * vs).astype(BF16)
    y = _dot(gated_ref[...], wout_ref[...])
    o_ref[...] = x + mod_ref[5:6, :] * y


def _sg_mixer(x, mod, layer, boff, seq, ng, w_in, b_in, ln_g, ln_b, w_s, bs_tab, w_out):
    t, d = x.shape
    half = w_out.shape[0]
    tm = min(ROW_TILE, seq)
    return pl.pallas_call(
        _sg_kernel,
        grid=(t // tm,),
        in_specs=[
            pl.BlockSpec((tm, d), lambda i: (i, 0)),
            _mod_spec(layer, boff, seq // tm, d),
            _const_spec((1, d)),
            _const_spec((d, 2 * half)),
            _const_spec((1, 2 * half)),
            _const_spec((1, half)),
            _const_spec((1, half)),
            _const_spec((SG_GROUPS, SG_CHUNK, SG_CHUNK)),
            _const_spec((SG_GROUPS, SG_CHUNK, LANES)),
            _const_spec((half, d)),
        ],
        out_specs=pl.BlockSpec((tm, d), lambda i: (i, 0)),
        out_shape=jax.ShapeDtypeStruct((t, d), F32),
        scratch_shapes=[pltpu.VMEM((tm, half), F32), pltpu.VMEM((tm, half), BF16),
                        pltpu.VMEM((tm, half), BF16)],
        compiler_params=_cparams(1),
        name="sg_mixer",
    )(x, mod, ng, w_in, b_in, ln_g, ln_b, w_s, bs_tab, w_out)


def _rope_tables(seq, head_dim):
    rot = head_dim // 4
    half = rot // 2
    pos = jnp.arange(seq, dtype=F32)
    inv_freq = ROPE_THETA ** (-jnp.arange(0, rot, 2, dtype=F32) / rot)
    ang = pos[:, None] * inv_freq[None, :]
    cos, sin = jnp.cos(ang), jnp.sin(ang)
    ones = jnp.ones((seq, head_dim - rot), F32)
    zeros_r = jnp.zeros((seq, head_dim - rot), F32)
    zeros_h = jnp.zeros((seq, half), F32)
    cos_t = jnp.concatenate([cos, cos, ones], axis=1)
    sin_prev = jnp.concatenate([zeros_h, sin, zeros_r], axis=1)
    sin_next = jnp.concatenate([-sin, zeros_h, zeros_r], axis=1)
    two = lambda a: jnp.concatenate([a, a], axis=1)
    return two(cos_t), two(sin_prev), two(sin_next)


def _qkv_kernel(x_ref, mod_ref, ng_ref, w_ref, cos_ref, sp_ref, sn_ref, q_ref, kt_ref, v_ref,
                *, q_scale, rot_half):
    x = x_ref[...]
    d = x.shape[1]
    h = _norm_mod(x, ng_ref[...], mod_ref, 3).astype(BF16)
    width = 2 * LANES
    cos_t = jnp.concatenate([cos_ref[...]] * 2, axis=1)
    sin_p = jnp.concatenate([sp_ref[...]] * 2, axis=1)
    sin_n = jnp.concatenate([sn_ref[...]] * 2, axis=1)

    def rotary(z):
        return (z * cos_t + pltpu.roll(z, rot_half, 1) * sin_p
                + pltpu.roll(z, width - rot_half, 1) * sin_n)

    for c in range(d // width):
        sl = slice(c * width, (c + 1) * width)
        q = rotary(_dot(h, w_ref[:, sl])) * q_scale
        q_ref[:, sl] = q.astype(BF16)
        k = rotary(_dot(h, w_ref[:, d + c * width:d + (c + 1) * width]))
        kt_ref[sl, :] = k.T.astype(BF16)
    v_ref[...] = _dot(h, w_ref[:, 2 * d:]).astype(BF16)


def _flash_kernel(q_ref, kt_ref, v_ref, lam_ref, sg_ref, o_ref, m_ref, l_ref, acc_ref,
                  *, lambda_init, head_dim):
    tq = q_ref.shape[0]
    seq = v_ref.shape[0]
    q = q_ref[...]
    lane = lax.broadcasted_iota(jnp.int32, q.shape, 1)
    zero = jnp.zeros_like(q)
    qq = jnp.concatenate([jnp.where(lane < head_dim, q, zero),
                          jnp.where(lane >= head_dim, q, zero)], axis=0)
    m_ref[...] = jnp.full(m_ref.shape, -1e30, F32)
    l_ref[...] = jnp.zeros(l_ref.shape, F32)
    acc_ref[...] = jnp.zeros(acc_ref.shape, F32)

    def body(j, carry):
        start = pl.multiple_of(j * KV_TILE, KV_TILE)
        s = _dot(qq, kt_ref[:, pl.ds(start, KV_TILE)])
        m_old = m_ref[...]
        m_new = jnp.maximum(m_old, jnp.max(s, axis=-1, keepdims=True))
        alpha = jnp.exp(m_old - m_new)
        p = jnp.exp(s - m_new)
        l_ref[...] = alpha * l_ref[...] + jnp.sum(p, axis=-1, keepdims=True)
        acc_ref[...] = alpha * acc_ref[...] + _dot(p.astype(BF16), v_ref[pl.ds(start, KV_TILE), :])
        m_ref[...] = m_new
        return carry

    lax.fori_loop(0, seq // KV_TILE, body, 0)

    lam = lam_ref[...]
    lam_full = (jnp.exp(jnp.sum(lam[0:1] * lam[1:2], axis=-1, keepdims=True))
                - jnp.exp(jnp.sum(lam[2:3] * lam[3:4], axis=-1, keepdims=True)) + lambda_init)
    a = acc_ref[...] / l_ref[...]
    o = a[:tq] - lam_full * a[tq:]
    o_ref[...] = (_rms(o, sg_ref[...]) * (1.0 - lambda_init)).astype(BF16)


def _proj_kernel(a_ref, x_ref, mod_ref, w_ref, o_ref):
    o_ref[...] = x_ref[...] + mod_ref[5:6, :] * _dot(a_ref[...], w_ref[...])


def _da_mixer(x, mod, layer, boff, bsz, seq, ng, w_qkv, lam, subln_g, w_out, lambda_init):
    t, d = x.shape
    head_w = d // DA_HEADS
    head_dim = head_w // 2
    tm = min(ROW_TILE, seq)
    tps = seq // tm
    cos_t, sin_p, sin_n = _rope_tables(seq, head_dim)
    tab_spec = pl.BlockSpec((tm, head_w), lambda i: (i % tps, 0))
    q, kt, v = pl.pallas_call(
        functools.partial(_qkv_kernel, q_scale=head_dim ** -0.5, rot_half=head_dim // 8),
        grid=(t // tm,),
        in_specs=[
            pl.BlockSpec((tm, d), lambda i: (i, 0)),
            _mod_spec(layer, boff, tps, d),
            _const_spec((1, d)),
            _const_spec((d, 3 * d)),
            tab_spec, tab_spec, tab_spec,
        ],
        out_specs=[
            pl.BlockSpec((tm, d), lambda i: (i, 0)),
            pl.BlockSpec((None, d, tm), lambda i: (i // tps, 0, i % tps)),
            pl.BlockSpec((tm, d), lambda i: (i, 0)),
        ],
        out_shape=[jax.ShapeDtypeStruct((t, d), BF16),
                   jax.ShapeDtypeStruct((bsz, d, seq), BF16),
                   jax.ShapeDtypeStruct((t, d), BF16)],
        compiler_params=_cparams(1),
        name="da_qkv",
    )(x, mod, ng, w_qkv, cos_t, sin_p, sin_n)

    tq = min(Q_TILE, seq)
    nq = seq // tq
    attn = pl.pallas_call(
        functools.partial(_flash_kernel, lambda_init=lambda_init, head_dim=head_dim),
        grid=(bsz, DA_HEADS, nq),
        in_specs=[
            pl.BlockSpec((tq, head_w), lambda b, hh, i: (b * nq + i, hh)),
            pl.BlockSpec((None, head_w, seq), lambda b, hh, i: (b, hh, 0)),
            pl.BlockSpec((seq, head_w), lambda b, hh, i: (b, hh)),
            pl.BlockSpec(lam.shape, lambda b, hh, i: (0, 0)),
            pl.BlockSpec((1, head_w), lambda b, hh, i: (0, 0)),
        ],
        out_specs=pl.BlockSpec((tq, head_w), lambda b, hh, i: (b * nq + i, hh)),
        out_shape=jax.ShapeDtypeStruct((t, d), BF16),
        scratch_shapes=[pltpu.VMEM((2 * tq, 1), F32), pltpu.VMEM((2 * tq, 1), F32),
                        pltpu.VMEM((2 * tq, head_w), F32)],
        compiler_params=_cparams(3),
        name="da_flash",
    )(q, kt, v, lam, subln_g)

    return pl.pallas_call(
        _proj_kernel,
        grid=(t // tm,),
        in_specs=[
            pl.BlockSpec((tm, d), lambda i: (i, 0)),
            pl.BlockSpec((tm, d), lambda i: (i, 0)),
            _mod_spec(layer, boff, tps, d),
            _const_spec((d, d)),
        ],
        out_specs=pl.BlockSpec((tm, d), lambda i: (i, 0)),
        out_shape=jax.ShapeDtypeStruct((t, d), F32),
        compiler_params=_cparams(1),
        name="da_out",
    )(attn, x, mod, w_out)


def _conv_kernel(xp_ref, x_ref, xn_ref, mod_ref, ng_ref, win_ref, ck_ref, wout_ref, o_ref, g_ref,
                 *, tiles_per_seq):
    i = pl.program_id(0)
    x = x_ref[...]
    tm, d = x.shape
    halo = xp_ref.shape[0]
    ng = ng_ref[...]
    xe = jnp.concatenate([xp_ref[...], x, xn_ref[...]], axis=0)
    h = _norm_mod(xe, ng, mod_ref, 3).astype(BF16)
    cg = _dot(h, win_ref[:, d:2 * d]) * _dot(h, win_ref[:, 2 * d:])
    row = lax.broadcasted_iota(jnp.int32, cg.shape, 0)
    first = (i % tiles_per_seq) == 0
    last = (i % tiles_per_seq) == tiles_per_seq - 1
    outside = jnp.logical_or(jnp.logical_and(first, row < halo),
                             jnp.logical_and(last, row >= halo + tm))
    g_ref[...] = jnp.where(outside, 0.0, cg)
    conv = (g_ref[halo - 1:halo - 1 + tm, :] * ck_ref[0:1, :]
            + g_ref[halo:halo + tm, :] * ck_ref[1:2, :]
            + g_ref[halo + 1:halo + 1 + tm, :] * ck_ref[2:3, :])
    bg = _dot(h[halo:halo + tm, :], win_ref[:, :d])
    y = _dot((bg * conv).astype(BF16), wout_ref[...])
    o_ref[...] = x + mod_ref[5:6, :] * y


def _conv_mixer(x, mod, layer, boff, seq, ng, w_in, conv_k, w_out):
    t, d = x.shape
    tm = min(ROW_TILE, seq)
    tps = seq // tm
    halo = BF16_ROWS
    hb = tm // halo
    n_halo_blocks = t // halo
    return pl.pallas_call(
        functools.partial(_conv_kernel, tiles_per_seq=tps),
        grid=(t // tm,),
        in_specs=[
            pl.BlockSpec((halo, d), lambda i: (jnp.maximum(i * hb - 1, 0), 0)),
            pl.BlockSpec((tm, d), lambda i: (i, 0)),
            pl.BlockSpec((halo, d), lambda i: (jnp.minimum((i + 1) * hb, n_halo_blocks - 1), 0)),
            _mod_spec(layer, boff, tps, d),
            _const_spec((1, d)),
            _const_spec((d, 3 * d)),
            _const_spec(conv_k.shape),
            _const_spec((d, d)),
        ],
        out_specs=pl.BlockSpec((tm, d), lambda i: (i, 0)),
        out_shape=jax.ShapeDtypeStruct((t, d), F32),
        scratch_shapes=[pltpu.VMEM((tm + 2 * halo, d), F32)],
        compiler_params=_cparams(1),
        name="conv_mixer",
    )(x, x, x, mod, ng, w_in, conv_k, w_out)


def _trunk(x, mod, boff, p):
    bsz, seq, d = x.shape
    depth = p["norm_g"].shape[0]
    x = x.reshape(bsz * seq, d)
    for i in range(depth):
        ng = p["norm_g"][i]
        x = _ffn(x, mod, i, boff, seq, ng[0:1], p["ffn_w_gate"][i, 0], p["ffn_w_up"][i, 0],
                 p["ffn_w_down"][i, 0], 0)
        j = i // N_MIXERS
        kind = i % N_MIXERS
        if kind == 0:
            x = _sg_mixer(x, mod, i, boff, seq, ng[1:2], p["sg_w_in"][j], p["sg_b_in"][j:j + 1],
                          p["sg_ln_g"][j:j + 1], p["sg_ln_b"][j:j + 1], p["sg_w_s"][j],
                          p["sg_bs_tab"][j], p["sg_w_out"][j])
        elif kind == 1:
            lambda_init = 0.8 - 0.6 * math.exp(-0.3 * i)
            x = _da_mixer(x, mod, i, boff, bsz, seq, ng[1:2], p["da_w_qkv"][j], p["da_lambda"][j],
                          p["da_subln_g"][j:j + 1], p["da_w_out"][j], lambda_init)
        else:
            x = _conv_mixer(x, mod, i, boff, seq, ng[1:2], p["conv_w_in"][j], p["conv_kernel"][j],
                            p["conv_w_out"][j])
        final_g = p["final_norm_g"].reshape(1, d) if i == depth - 1 else None
        x = _ffn(x, mod, i, boff, seq, ng[2:3], p["ffn_w_gate"][i, 1], p["ffn_w_up"][i, 1],
                 p["ffn_w_down"][i, 1], 6, final_g)
    return x.reshape(bsz, seq, d)


def kernel(x_prompt, x_sample, c_prompt, c_sample, norm_g, ada_w, ada_b, ffn_w_gate, ffn_w_up, ffn_w_down, sg_w_in, sg_b_in, sg_ln_g, sg_ln_b, sg_w_s, sg_b_s, sg_w_out, da_w_qkv, da_lambda, da_subln_g, da_w_out, conv_w_in, conv_kernel, conv_w_out, final_norm_g):
    bf = lambda w: w.astype(BF16)
    p = dict(
        norm_g=norm_g, final_norm_g=final_norm_g,
        ffn_w_gate=bf(ffn_w_gate), ffn_w_up=bf(ffn_w_up), ffn_w_down=bf(ffn_w_down),
        sg_w_in=bf(sg_w_in), sg_b_in=sg_b_in, sg_ln_g=sg_ln_g, sg_ln_b=sg_ln_b, sg_w_s=bf(sg_w_s),
        sg_bs_tab=jnp.broadcast_to(sg_b_s[..., None], sg_b_s.shape + (LANES,)),
        sg_w_out=bf(sg_w_out),
        da_w_qkv=bf(da_w_qkv), da_lambda=da_lambda, da_subln_g=da_subln_g, da_w_out=bf(da_w_out),
        conv_w_in=bf(conv_w_in), conv_kernel=conv_kernel, conv_w_out=bf(conv_w_out),
    )
    mod = _ada_mod(jnp.concatenate([c_prompt, c_sample], axis=0), ada_w, ada_b)
    y_prompt = _trunk(x_prompt, mod, 0, p)
    y_sample = _trunk(x_sample, mod, c_prompt.shape[0], p)
    return (y_prompt, y_sample)
```

```python
import functools
import math

import jax
import jax.numpy as jnp
from jax import lax
from jax.experimental import pallas as pl
from jax.experimental.pallas import tpu as pltpu

F32 = jnp.float32
BF16 = jnp.bfloat16
EPS = 1e-6

N_ADA = 9
N_MIXERS = 3
SG_CHUNK = 128
SG_GROUPS = 8
DA_HEADS = 8
ROPE_THETA = 500000.0
V7X_VMEM_LIMIT_BYTES = 56 * 1024 * 1024
LANES = 128
BF16_ROWS = 16

ROW_TILE = 512
FFN_COL_CHUNK = 256
Q_TILE = 256
KV_TILE = 512
FLASH_PAIR_UNROLL = 3


def _cparams(n_axes):
    return pltpu.CompilerParams(
        dimension_semantics=("arbitrary",) * n_axes,
        vmem_limit_bytes=V7X_VMEM_LIMIT_BYTES,
    )


def _const_spec(shape):
    zeros = (0,) * len(shape)
    return pl.BlockSpec(shape, lambda *_: zeros, pipeline_mode=pl.Buffered(1))


def _rms(x, g):
    ms = jnp.mean(x * x, axis=-1, keepdims=True)
    return x * lax.rsqrt(ms + EPS) * g


def _norm_mod(x, g, mod_ref, row):
    shift = mod_ref[row:row + 1, :]
    scale = mod_ref[row + 1:row + 2, :]
    return _rms(x, g) * (1.0 + scale) + shift


def _silu(x):
    return x * jax.nn.sigmoid(x)


def _gelu(x):
    return 0.5 * x * (1.0 + lax.erf(x * math.sqrt(0.5)))


def _dot(a, b):
    return jnp.dot(a, b, preferred_element_type=F32)


def _ada_kernel(c_ref, w_ref, b_ref, o_ref):
    a = _silu(c_ref[...]).astype(BF16)
    o_ref[...] = _dot(a, w_ref[...].astype(BF16)) + b_ref[...]


def _ada_mod(c_all, ada_w, ada_b):
    depth, d, nd = ada_w.shape
    bt = c_all.shape[0]
    tn = d
    out = pl.pallas_call(
        _ada_kernel,
        grid=(depth, nd // tn),
        in_specs=[
            pl.BlockSpec((bt, d), lambda l, j: (0, 0)),
            pl.BlockSpec((None, d, tn), lambda l, j: (l, 0, j)),
            pl.BlockSpec((None, 1, tn), lambda l, j: (l, 0, j)),
        ],
        out_specs=pl.BlockSpec((None, bt, tn), lambda l, j: (l, 0, j)),
        out_shape=jax.ShapeDtypeStruct((depth, bt, nd), F32),
        compiler_params=_cparams(2),
        name="ada_mod",
    )(c_all, ada_w, ada_b.reshape(depth, 1, nd))
    return out.reshape(depth, bt, N_ADA, d)


def _mod_spec(layer, boff, tiles_per_seq, d):
    return pl.BlockSpec((None, None, N_ADA, d),
                        lambda i, *_: (layer, boff + i // tiles_per_seq, 0, 0))


def _ffn_kernel(x_ref, mod_ref, ng_ref, wg_ref, wu_ref, wd_ref, *rest, mod_row, final):
    if final:
        fg_ref, o_ref, act_ref = rest
    else:
        o_ref, act_ref = rest
    x = x_ref[...]
    h = _norm_mod(x, ng_ref[...], mod_ref, mod_row).astype(BF16)
    d_ff = wg_ref.shape[1]
    for c in range(d_ff // FFN_COL_CHUNK):
        sl = slice(c * FFN_COL_CHUNK, (c + 1) * FFN_COL_CHUNK)
        g = _dot(h, wg_ref[:, sl])
        u = _dot(h, wu_ref[:, sl])
        act_ref[:, sl] = (_silu(g) * u).astype(BF16)
    y = _dot(act_ref[...], wd_ref[...])
    gate = mod_ref[mod_row + 2:mod_row + 3, :]
    out = x + (0.5 * gate) * y
    if final:
        out = _rms(out, fg_ref[...])
    o_ref[...] = out


def _ffn(x, mod, layer, boff, seq, ng, wg, wu, wd, mod_row, final_g=None):
    t, d = x.shape
    d_ff = wg.shape[1]
    tm = min(ROW_TILE, seq)
    final = final_g is not None
    in_specs = [
        pl.BlockSpec((tm, d), lambda i: (i, 0)),
        _mod_spec(layer, boff, seq // tm, d),
        _const_spec((1, d)),
        _const_spec((d, d_ff)),
        _const_spec((d, d_ff)),
        _const_spec((d_ff, d)),
    ]
    args = [x, mod, ng, wg, wu, wd]
    if final:
        in_specs.append(_const_spec((1, d)))
        args.append(final_g)
    return pl.pallas_call(
        functools.partial(_ffn_kernel, mod_row=mod_row, final=final),
        grid=(t // tm,),
        in_specs=in_specs,
        out_specs=pl.BlockSpec((tm, d), lambda i: (i, 0)),
        out_shape=jax.ShapeDtypeStruct((t, d), F32),
        scratch_shapes=[pltpu.VMEM((tm, d_ff), BF16)],
        compiler_params=_cparams(1),
        name="ffn",
    )(*args)


def _sg_kernel(x_ref, mod_ref, ng_ref, win_ref, bin_ref, lng_ref, lnb_ref, ws_ref, bs_ref,
               wout_ref, o_ref, v_ref, vn_ref, gated_ref):
    x = x_ref[...]
    tm = x.shape[0]
    half = wout_ref.shape[0]
    gdim = half // SG_GROUPS
    h = _norm_mod(x, ng_ref[...], mod_ref, 3).astype(BF16)
    for g in range(SG_GROUPS):
        sl = slice(g * gdim, (g + 1) * gdim)
        sv = slice(half + g * gdim, half + (g + 1) * gdim)
        v_ref[:, sl] = _gelu(_dot(h, win_ref[:, sv]) + bin_ref[:, sv])
    v = v_ref[...]
    mu = jnp.mean(v, axis=-1, keepdims=True)
    vc = v - mu
    var = jnp.mean(vc * vc, axis=-1, keepdims=True)
    vn_ref[...] = (vc * lax.rsqrt(var + EPS) * lng_ref[...] + lnb_ref[...]).astype(BF16)
    for g in range(SG_GROUPS):
        sl = slice(g * gdim, (g + 1) * gdim)
        u = _gelu(_dot(h, win_ref[:, sl]) + bin_ref[:, sl])
        bias = jnp.concatenate([bs_ref[g]] * (gdim // LANES), axis=1)
        for n in range(tm // SG_CHUNK):
            rows = slice(n * SG_CHUNK, (n + 1) * SG_CHUNK)
            vs = _dot(ws_ref[g], vn_ref[rows, sl]) + bias
            gated_ref[rows, sl] = (u[rows, :] * vs).astype(BF16)
    y = _dot(gated_ref[...], wout_ref[...])
    o_ref[...] = x + mod_ref[5:6, :] * y


def _sg_mixer(x, mod, layer, boff, seq, ng, w_in, b_in, ln_g, ln_b, w_s, bs_tab, w_out):
    t, d = x.shape
    half = w_out.shape[0]
    tm = min(ROW_TILE, seq)
    return pl.pallas_call(
        _sg_kernel,
        grid=(t // tm,),
        in_specs=[
            pl.BlockSpec((tm, d), lambda i: (i, 0)),
            _mod_spec(layer, boff, seq // tm, d),
            _const_spec((1, d)),
            _const_spec((d, 2 * half)),
            _const_spec((1, 2 * half)),
            _const_spec((1, half)),
            _const_spec((1, half)),
            _const_spec((SG_GROUPS, SG_CHUNK, SG_CHUNK)),
            _const_spec((SG_GROUPS, SG_CHUNK, LANES)),
            _const_spec((half, d)),
        ],
        out_specs=pl.BlockSpec((tm, d), lambda i: (i, 0)),
        out_shape=jax.ShapeDtypeStruct((t, d), F32),
        scratch_shapes=[pltpu.VMEM((tm, half), F32), pltpu.VMEM((tm, half), BF16),
                        pltpu.VMEM((tm, half), BF16)],
        compiler_params=_cparams(1),
        name="sg_mixer",
    )(x, mod, ng, w_in, b_in, ln_g, ln_b, w_s, bs_tab, w_out)


def _rope_tables(seq, head_dim):
    rot = head_dim // 4
    half = rot // 2
    pos = jnp.arange(seq, dtype=F32)
    inv_freq = ROPE_THETA ** (-jnp.arange(0, rot, 2, dtype=F32) / rot)
    ang = pos[:, None] * inv_freq[None, :]
    cos, sin = jnp.cos(ang), jnp.sin(ang)
    ones = jnp.ones((seq, head_dim - rot), F32)
    zeros_r = jnp.zeros((seq, head_dim - rot), F32)
    zeros_h = jnp.zeros((seq, half), F32)
    cos_t = jnp.concatenate([cos, cos, ones], axis=1)
    sin_prev = jnp.concatenate([zeros_h, sin, zeros_r], axis=1)
    sin_next = jnp.concatenate([-sin, zeros_h, zeros_r], axis=1)
    two = lambda a: jnp.concatenate([a, a], axis=1)
    return two(cos_t), two(sin_prev), two(sin_next)


def _qkv_kernel(x_ref, mod_ref, ng_ref, w_ref, cos_ref, sp_ref, sn_ref, qt_ref, k_ref, vt_ref,
                *, q_scale, rot_half):
    x = x_ref[...]
    d = x.shape[1]
    h = _norm_mod(x, ng_ref[...], mod_ref, 3).astype(BF16)
    width = 2 * LANES
    cos_t = jnp.concatenate([cos_ref[...]] * 2, axis=1)
    sin_p = jnp.concatenate([sp_ref[...]] * 2, axis=1)
    sin_n = jnp.concatenate([sn_ref[...]] * 2, axis=1)

    def rotary(z):
        return (z * cos_t + pltpu.roll(z, rot_half, 1) * sin_p
                + pltpu.roll(z, width - rot_half, 1) * sin_n)

    for c in range(d // width):
        sl = slice(c * width, (c + 1) * width)
        q = rotary(_dot(h, w_ref[:, sl])) * q_scale
        qt_ref[sl, :] = q.T.astype(BF16)
        k = rotary(_dot(h, w_ref[:, d + c * width:d + (c + 1) * width]))
        k_ref[:, sl] = k.astype(BF16)
        v = _dot(h, w_ref[:, 2 * d + c * width:2 * d + (c + 1) * width])
        vt_ref[sl, :] = v.T.astype(BF16)


def _flash_kernel(qt_ref, k_ref, vt_ref, lam_ref, sg_ref, o_ref, acc_ref, s0_ref, s1_ref,
                  *, lambda_init, head_dim):
    tq = qt_ref.shape[1]
    seq = k_ref.shape[0]
    tk = s0_ref.shape[0]
    n_pairs = seq // (2 * tk)
    qt = qt_ref[...]
    row = lax.broadcasted_iota(jnp.int32, qt.shape, 0)
    zero = jnp.zeros_like(qt)
    qqt = jnp.concatenate([jnp.where(row < head_dim, qt, zero),
                           jnp.where(row >= head_dim, qt, zero)], axis=1)
    acc_ref[...] = jnp.zeros(acc_ref.shape, F32)

    def aligned(start):
        return start if isinstance(start, int) else pl.multiple_of(start, tk)

    def scores(start, s_ref):
        st = _dot(k_ref[pl.ds(aligned(start), tk), :], qqt)
        s_ref[...] = st
        return jnp.max(st, axis=0, keepdims=True)

    def accumulate(start, s_ref, m_old, l_old, m_tile):
        m_new = jnp.maximum(m_old, m_tile)
        alpha = jnp.exp2(m_old - m_new)
        pt = jnp.exp2(s_ref[...] - m_new)
        l_new = alpha * l_old + jnp.sum(pt, axis=0, keepdims=True)
        acc_ref[...] = alpha * acc_ref[...] + _dot(vt_ref[:, pl.ds(aligned(start), tk)],
                                                   pt.astype(BF16))
        return m_new, l_new

    def pair(i, carry):
        m, l, m_even = carry
        base = i * (2 * tk)
        m_odd = scores(base + tk, s1_ref)
        m, l = accumulate(base, s0_ref, m, l, m_even)
        m_even = scores(base + 2 * tk, s0_ref)
        m, l = accumulate(base + tk, s1_ref, m, l, m_odd)
        return m, l, m_even

    m0 = jnp.full((1, 2 * tq), -1e30, F32)
    l0 = jnp.zeros((1, 2 * tq), F32)
    m, l, m_even = lax.fori_loop(0, n_pairs - 1, pair, (m0, l0, scores(0, s0_ref)),
                                 unroll=FLASH_PAIR_UNROLL)
    last = seq - 2 * tk
    m_odd = scores(last + tk, s1_ref)
    m, l = accumulate(last, s0_ref, m, l, m_even)
    _, l_fin = accumulate(last + tk, s1_ref, m, l, m_odd)

    lam = lam_ref[...]
    lam_full = (jnp.exp(jnp.sum(lam[0:1] * lam[1:2], axis=-1, keepdims=True))
                - jnp.exp(jnp.sum(lam[2:3] * lam[3:4], axis=-1, keepdims=True)) + lambda_init)
    a = acc_ref[...] / l_fin
    o = a[:, :tq] - lam_full * a[:, tq:]
    ms = jnp.mean(o * o, axis=0, keepdims=True)
    gain = jnp.concatenate([sg_ref[...]] * (tq // LANES), axis=1)
    o = o * lax.rsqrt(ms + EPS) * gain * (1.0 - lambda_init)
    o_ref[...] = o.T.astype(BF16)


def _proj_kernel(a_ref, x_ref, mod_ref, w_ref, o_ref):
    o_ref[...] = x_ref[...] + mod_ref[5:6, :] * _dot(a_ref[...], w_ref[...])


def _da_mixer(x, mod, layer, boff, bsz, seq, ng, w_qkv, lam, subln_g, w_out, lambda_init):
    t, d = x.shape
    head_w = d // DA_HEADS
    head_dim = head_w // 2
    tm = min(ROW_TILE, seq)
    tps = seq // tm
    cos_t, sin_p, sin_n = _rope_tables(seq, head_dim)
    tab_spec = pl.BlockSpec((tm, head_w), lambda i: (i % tps, 0))
    seq_t_spec = pl.BlockSpec((None, d, tm), lambda i: (i // tps, 0, i % tps))
    qt, k, vt = pl.pallas_call(
        functools.partial(_qkv_kernel, q_scale=head_dim ** -0.5 * math.log2(math.e),
                          rot_half=head_dim // 8),
        grid=(t // tm,),
        in_specs=[
            pl.BlockSpec((tm, d), lambda i: (i, 0)),
            _mod_spec(layer, boff, tps, d),
            _const_spec((1, d)),
            _const_spec((d, 3 * d)),
            tab_spec, tab_spec, tab_spec,
        ],
        out_specs=[seq_t_spec, pl.BlockSpec((tm, d), lambda i: (i, 0)), seq_t_spec],
        out_shape=[jax.ShapeDtypeStruct((bsz, d, seq), BF16),
                   jax.ShapeDtypeStruct((t, d), BF16),
                   jax.ShapeDtypeStruct((bsz, d, seq), BF16)],
        compiler_params=_cparams(1),
        name="da_qkv",
    )(x, mod, ng, w_qkv, cos_t, sin_p, sin_n)

    tq = min(Q_TILE, seq)
    nq = seq // tq
    tk = min(KV_TILE, seq // 2)
    assert seq % (2 * tk) == 0 and seq % tq == 0
    gain_tab = jnp.broadcast_to(subln_g[:, None], (head_w, LANES))
    attn = pl.pallas_call(
        functools.partial(_flash_kernel, lambda_init=lambda_init, head_dim=head_dim),
        grid=(bsz, DA_HEADS, nq),
        in_specs=[
            pl.BlockSpec((None, head_w, tq), lambda b, hh, i: (b, hh, i)),
            pl.BlockSpec((seq, head_w), lambda b, hh, i: (b, hh)),
            pl.BlockSpec((None, head_w, seq), lambda b, hh, i: (b, hh, 0)),
            pl.BlockSpec(lam.shape, lambda b, hh, i: (0, 0)),
            pl.BlockSpec((head_w, LANES), lambda b, hh, i: (0, 0)),
        ],
        out_specs=pl.BlockSpec((tq, head_w), lambda b, hh, i: (b * nq + i, hh)),
        out_shape=jax.ShapeDtypeStruct((t, d), BF16),
        scratch_shapes=[pltpu.VMEM((head_w, 2 * tq), F32),
                        pltpu.VMEM((tk, 2 * tq), F32), pltpu.VMEM((tk, 2 * tq), F32)],
        compiler_params=_cparams(3),
        name="da_flash",
    )(qt, k, vt, lam, gain_tab)

    return pl.pallas_call(
        _proj_kernel,
        grid=(t // tm,),
        in_specs=[
            pl.BlockSpec((tm, d), lambda i: (i, 0)),
            pl.BlockSpec((tm, d), lambda i: (i, 0)),
            _mod_spec(layer, boff, tps, d),
            _const_spec((d, d)),
        ],
        out_specs=pl.BlockSpec((tm, d), lambda i: (i, 0)),
        out_shape=jax.ShapeDtypeStruct((t, d), F32),
        compiler_params=_cparams(1),
        name="da_out",
    )(attn, x, mod, w_out)


def _conv_kernel(xp_ref, x_ref, xn_ref, mod_ref, ng_ref, win_ref, ck_ref, wout_ref, o_ref, g_ref,
                 *, tiles_per_seq):
    i = pl.program_id(0)
    x = x_ref[...]
    tm, d = x.shape
    halo = xp_ref.shape[0]
    ng = ng_ref[...]
    xe = jnp.concatenate([xp_ref[...], x, xn_ref[...]], axis=0)
    h = _norm_mod(xe, ng, mod_ref, 3).astype(BF16)
    cg = _dot(h, win_ref[:, d:2 * d]) * _dot(h, win_ref[:, 2 * d:])
    row = lax.broadcasted_iota(jnp.int32, cg.shape, 0)
    first = (i % tiles_per_seq) == 0
    last = (i % tiles_per_seq) == tiles_per_seq - 1
    outside = jnp.logical_or(jnp.logical_and(first, row < halo),
                             jnp.logical_and(last, row >= halo + tm))
    g_ref[...] = jnp.where(outside, 0.0, cg)
    conv = (g_ref[halo - 1:halo - 1 + tm, :] * ck_ref[0:1, :]
            + g_ref[halo:halo + tm, :] * ck_ref[1:2, :]
            + g_ref[halo + 1:halo + 1 + tm, :] * ck_ref[2:3, :])
    bg = _dot(h[halo:halo + tm, :], win_ref[:, :d])
    y = _dot((bg * conv).astype(BF16), wout_ref[...])
    o_ref[...] = x + mod_ref[5:6, :] * y


def _conv_mixer(x, mod, layer, boff, seq, ng, w_in, conv_k, w_out):
    t, d = x.shape
    tm = min(ROW_TILE, seq)
    tps = seq // tm
    halo = BF16_ROWS
    hb = tm // halo
    n_halo_blocks = t // halo
    return pl.pallas_call(
        functools.partial(_conv_kernel, tiles_per_seq=tps),
        grid=(t // tm,),
        in_specs=[
            pl.BlockSpec((halo, d), lambda i: (jnp.maximum(i * hb - 1, 0), 0)),
            pl.BlockSpec((tm, d), lambda i: (i, 0)),
            pl.BlockSpec((halo, d), lambda i: (jnp.minimum((i + 1) * hb, n_halo_blocks - 1), 0)),
            _mod_spec(layer, boff, tps, d),
            _const_spec((1, d)),
            _const_spec((d, 3 * d)),
            _const_spec(conv_k.shape),
            _const_spec((d, d)),
        ],
        out_specs=pl.BlockSpec((tm, d), lambda i: (i, 0)),
        out_shape=jax.ShapeDtypeStruct((t, d), F32),
        scratch_shapes=[pltpu.VMEM((tm + 2 * halo, d), F32)],
        compiler_params=_cparams(1),
        name="conv_mixer",
    )(x, x, x, mod, ng, w_in, conv_k, w_out)


def _trunk(x, mod, boff, p):
    bsz, seq, d = x.shape
    depth = p["norm_g"].shape[0]
    x = x.reshape(bsz * seq, d)
    for i in range(depth):
        ng = p["norm_g"][i]
        x = _ffn(x, mod, i, boff, seq, ng[0:1], p["ffn_w_gate"][i, 0], p["ffn_w_up"][i, 0],
                 p["ffn_w_down"][i, 0], 0)
        j = i // N_MIXERS
        kind = i % N_MIXERS
        if kind == 0:
            x = _sg_mixer(x, mod, i, boff, seq, ng[1:2], p["sg_w_in"][j], p["sg_b_in"][j:j + 1],
                          p["sg_ln_g"][j:j + 1], p["sg_ln_b"][j:j + 1], p["sg_w_s"][j],
                          p["sg_bs_tab"][j], p["sg_w_out"][j])
        elif kind == 1:
            lambda_init = 0.8 - 0.6 * math.exp(-0.3 * i)
            x = _da_mixer(x, mod, i, boff, bsz, seq, ng[1:2], p["da_w_qkv"][j], p["da_lambda"][j],
                          p["da_subln_g"][j], p["da_w_out"][j], lambda_init)
        else:
            x = _conv_mixer(x, mod, i, boff, seq, ng[1:2], p["conv_w_in"][j], p["conv_kernel"][j],
                            p["conv_w_out"][j])
        final_g = p["final_norm_g"].reshape(1, d) if i == depth - 1 else None
        x = _ffn(x, mod, i, boff, seq, ng[2:3], p["ffn_w_gate"][i, 1], p["ffn_w_up"][i, 1],
                 p["ffn_w_down"][i, 1], 6, final_g)
    return x.reshape(bsz, seq, d)


def kernel(x_prompt, x_sample, c_prompt, c_sample, norm_g, ada_w, ada_b, ffn_w_gate, ffn_w_up, ffn_w_down, sg_w_in, sg_b_in, sg_ln_g, sg_ln_b, sg_w_s, sg_b_s, sg_w_out, da_w_qkv, da_lambda, da_subln_g, da_w_out, conv_w_in, conv_kernel, conv_w_out, final_norm_g):
    bf = lambda w: w.astype(BF16)
    p = dict(
        norm_g=norm_g, final_norm_g=final_norm_g,
        ffn_w_gate=bf(ffn_w_gate), ffn_w_up=bf(ffn_w_up), ffn_w_down=bf(ffn_w_down),
        sg_w_in=bf(sg_w_in), sg_b_in=sg_b_in, sg_ln_g=sg_ln_g, sg_ln_b=sg_ln_b, sg_w_s=bf(sg_w_s),
        sg_bs_tab=jnp.broadcast_to(sg_b_s[..., None], sg_b_s.shape + (LANES,)),
        sg_w_out=bf(sg_w_out),
        da_w_qkv=bf(da_w_qkv), da_lambda=da_lambda, da_subln_g=da_subln_g, da_w_out=bf(da_w_out),
        conv_w_in=bf(conv_w_in), conv_kernel=conv_kernel, conv_w_out=bf(conv_w_out),
    )
    mod = _ada_mod(jnp.concatenate([c_prompt, c_sample], axis=0), ada_w, ada_b)
    y_prompt = _trunk(x_prompt, mod, 0, p)
    y_sample = _trunk(x_sample, mod, c_prompt.shape[0], p)
    return (y_prompt, y_sample)
```

```python
import functools
import math

import jax
import jax.numpy as jnp
from jax import lax
from jax.experimental import pallas as pl
from jax.experimental.pallas import tpu as pltpu

F32 = jnp.float32
BF16 = jnp.bfloat16
EPS = 1e-6

N_ADA = 9
N_MIXERS = 3
SG_CHUNK = 128
SG_GROUPS = 8
DA_HEADS = 8
ROPE_THETA = 500000.0
V7X_VMEM_LIMIT_BYTES = 56 * 1024 * 1024
LANES = 128
BF16_ROWS = 16

ROW_TILE = 512
FFN_COL_CHUNK = 256
Q_TILE = 256
KV_TILE = 512
FLASH_PAIR_UNROLL = 3


def _cparams(n_axes):
    return pltpu.CompilerParams(
        dimension_semantics=("arbitrary",) * n_axes,
        vmem_limit_bytes=V7X_VMEM_LIMIT_BYTES,
    )


def _const_spec(shape):
    zeros = (0,) * len(shape)
    return pl.BlockSpec(shape, lambda *_: zeros, pipeline_mode=pl.Buffered(1))


def _rms(x, g):
    ms = jnp.mean(x * x, axis=-1, keepdims=True)
    return x * lax.rsqrt(ms + EPS) * g


def _norm_mod(x, g, mod_ref, row):
    shift = mod_ref[row:row + 1, :]
    scale = mod_ref[row + 1:row + 2, :]
    return _rms(x, g) * (1.0 + scale) + shift


def _silu(x):
    return x * jax.nn.sigmoid(x)


def _gelu(x):
    return 0.5 * x * (1.0 + lax.erf(x * math.sqrt(0.5)))


def _dot(a, b):
    return jnp.dot(a, b, preferred_element_type=F32)


def _ada_kernel(c_ref, w_ref, b_ref, o_ref):
    a = _silu(c_ref[...]).astype(BF16)
    o_ref[...] = _dot(a, w_ref[...].astype(BF16)) + b_ref[...]


def _ada_mod(c_all, ada_w, ada_b):
    depth, d, nd = ada_w.shape
    bt = c_all.shape[0]
    tn = d
    out = pl.pallas_call(
        _ada_kernel,
        grid=(depth, nd // tn),
        in_specs=[
            pl.BlockSpec((bt, d), lambda l, j: (0, 0)),
            pl.BlockSpec((None, d, tn), lambda l, j: (l, 0, j)),
            pl.BlockSpec((None, 1, tn), lambda l, j: (l, 0, j)),
        ],
        out_specs=pl.BlockSpec((None, bt, tn), lambda l, j: (l, 0, j)),
        out_shape=jax.ShapeDtypeStruct((depth, bt, nd), F32),
        compiler_params=_cparams(2),
        name="ada_mod",
    )(c_all, ada_w, ada_b.reshape(depth, 1, nd))
    return out.reshape(depth, bt, N_ADA, d)


def _mod_spec(layer, boff, tiles_per_seq, d):
    return pl.BlockSpec((None, None, N_ADA, d),
                        lambda i, *_: (layer, boff + i // tiles_per_seq, 0, 0))


def _ffn_kernel(x_ref, mod_ref, ng_ref, wg_ref, wu_ref, wd_ref, *rest, mod_row, final):
    if final:
        fg_ref, o_ref, act_ref = rest
    else:
        o_ref, act_ref = rest
    x = x_ref[...]
    h = _norm_mod(x, ng_ref[...], mod_ref, mod_row).astype(BF16)
    d_ff = wg_ref.shape[1]
    for c in range(d_ff // FFN_COL_CHUNK):
        sl = slice(c * FFN_COL_CHUNK, (c + 1) * FFN_COL_CHUNK)
        g = _dot(h, wg_ref[:, sl])
        u = _dot(h, wu_ref[:, sl])
        act_ref[:, sl] = (_silu(g) * u).astype(BF16)
    y = _dot(act_ref[...], wd_ref[...])
    gate = mod_ref[mod_row + 2:mod_row + 3, :]
    out = x + (0.5 * gate) * y
    if final:
        out = _rms(out, fg_ref[...])
    o_ref[...] = out


def _ffn(x, mod, layer, boff, seq, ng, wg, wu, wd, mod_row, final_g=None):
    t, d = x.shape
    d_ff = wg.shape[1]
    tm = min(ROW_TILE, seq)
    final = final_g is not None
    in_specs = [
        pl.BlockSpec((tm, d), lambda i: (i, 0)),
        _mod_spec(layer, boff, seq // tm, d),
        _const_spec((1, d)),
        _const_spec((d, d_ff)),
        _const_spec((d, d_ff)),
        _const_spec((d_ff, d)),
    ]
    args = [x, mod, ng, wg, wu, wd]
    if final:
        in_specs.append(_const_spec((1, d)))
        args.append(final_g)
    return pl.pallas_call(
        functools.partial(_ffn_kernel, mod_row=mod_row, final=final),
        grid=(t // tm,),
        in_specs=in_specs,
        out_specs=pl.BlockSpec((tm, d), lambda i: (i, 0)),
        out_shape=jax.ShapeDtypeStruct((t, d), F32),
        scratch_shapes=[pltpu.VMEM((tm, d_ff), BF16)],
        compiler_params=_cparams(1),
        name="ffn",
    )(*args)


def _sg_kernel(x_ref, mod_ref, ng_ref, win_ref, bin_ref, lng_ref, lnb_ref, ws_ref, bs_ref,
               wout_ref, o_ref, v_ref, vn_ref, gated_ref):
    x = x_ref[...]
    tm = x.shape[0]
    half = wout_ref.shape[0]
    gdim = half // SG_GROUPS
    h = _norm_mod(x, ng_ref[...], mod_ref, 3).astype(BF16)
    n_chunks = tm // SG_CHUNK
    pair_w = 2 * gdim
    for c in range(SG_GROUPS // 2):
        sl = slice(c * pair_w, (c + 1) * pair_w)
        sv = slice(half + c * pair_w, half + (c + 1) * pair_w)
        v_ref[:, sl] = _gelu(_dot(h, win_ref[:, sv]) + bin_ref[:, sv])
    v = v_ref[...]
    mu = jnp.mean(v, axis=-1, keepdims=True)
    vc = v - mu
    var = jnp.mean(vc * vc, axis=-1, keepdims=True)
    vn_ref[...] = (vc * lax.rsqrt(var + EPS) * lng_ref[...] + lnb_ref[...]).astype(BF16)
    for c in range(SG_GROUPS // 2):
        u2 = _gelu(_dot(h, win_ref[:, c * pair_w:(c + 1) * pair_w])
                   + bin_ref[:, c * pair_w:(c + 1) * pair_w])
        for gg in range(2):
            g = 2 * c + gg
            sl = slice(g * gdim, (g + 1) * gdim)
            v_cat = jnp.concatenate(
                [vn_ref[n * SG_CHUNK:(n + 1) * SG_CHUNK, sl] for n in range(n_chunks)], axis=1)
            bias = jnp.concatenate([bs_ref[g]] * (gdim // LANES), axis=1)
            vs_cat = _dot(ws_ref[g], v_cat)
            for n in range(n_chunks):
                rows = slice(n * SG_CHUNK, (n + 1) * SG_CHUNK)
                vs = vs_cat[:, n * gdim:(n + 1) * gdim] + bias
                gated_ref[rows, sl] = (u2[rows, gg * gdim:(gg + 1) * gdim] * vs).astype(BF16)
    y = _dot(gated_ref[...], wout_ref[...])
    o_ref[...] = x + mod_ref[5:6, :] * y


def _sg_mixer(x, mod, layer, boff, seq, ng, w_in, b_in, ln_g, ln_b, w_s, bs_tab, w_out):
    t, d = x.shape
    half = w_out.shape[0]
    tm = min(ROW_TILE, seq)
    return pl.pallas_call(
        _sg_kernel,
        grid=(t // tm,),
        in_specs=[
            pl.BlockSpec((tm, d), lambda i: (i, 0)),
            _mod_spec(layer, boff, seq // tm, d),
            _const_spec((1, d)),
            _const_spec((d, 2 * half)),
            _const_spec((1, 2 * half)),
            _const_spec((1, half)),
            _const_spec((1, half)),
            _const_spec((SG_GROUPS, SG_CHUNK, SG_CHUNK)),
            _const_spec((SG_GROUPS, SG_CHUNK, LANES)),
            _const_spec((half, d)),
        ],
        out_specs=pl.BlockSpec((tm, d), lambda i: (i, 0)),
        out_shape=jax.ShapeDtypeStruct((t, d), F32),
        scratch_shapes=[pltpu.VMEM((tm, half), F32), pltpu.VMEM((tm, half), BF16),
                        pltpu.VMEM((tm, half), BF16)],
        compiler_params=_cparams(1),
        name="sg_mixer",
    )(x, mod, ng, w_in, b_in, ln_g, ln_b, w_s, bs_tab, w_out)


def _rope_tables(seq, head_dim):
    rot = head_dim // 4
    half = rot // 2
    pos = jnp.arange(seq, dtype=F32)
    inv_freq = ROPE_THETA ** (-jnp.arange(0, rot, 2, dtype=F32) / rot)
    ang = pos[:, None] * inv_freq[None, :]
    cos, sin = jnp.cos(ang), jnp.sin(ang)
    ones = jnp.ones((seq, head_dim - rot), F32)
    zeros_r = jnp.zeros((seq, head_dim - rot), F32)
    zeros_h = jnp.zeros((seq, half), F32)
    cos_t = jnp.concatenate([cos, cos, ones], axis=1)
    sin_prev = jnp.concatenate([zeros_h, sin, zeros_r], axis=1)
    sin_next = jnp.concatenate([-sin, zeros_h, zeros_r], axis=1)
    two = lambda a: jnp.concatenate([a, a], axis=1)
    return two(cos_t), two(sin_prev), two(sin_next)


def _qkv_kernel(x_ref, mod_ref, ng_ref, w_ref, cos_ref, sp_ref, sn_ref, qt_ref, k_ref, vt_ref,
                *, q_scale, rot_half):
    x = x_ref[...]
    d = x.shape[1]
    h = _norm_mod(x, ng_ref[...], mod_ref, 3).astype(BF16)
    width = 2 * LANES
    cos_t = jnp.concatenate([cos_ref[...]] * 2, axis=1)
    sin_p = jnp.concatenate([sp_ref[...]] * 2, axis=1)
    sin_n = jnp.concatenate([sn_ref[...]] * 2, axis=1)

    def rotary(z):
        return (z * cos_t + pltpu.roll(z, rot_half, 1) * sin_p
                + pltpu.roll(z, width - rot_half, 1) * sin_n)

    for c in range(d // width):
        sl = slice(c * width, (c + 1) * width)
        q = rotary(_dot(h, w_ref[:, sl])) * q_scale
        qt_ref[sl, :] = q.T.astype(BF16)
        k = rotary(_dot(h, w_ref[:, d + c * width:d + (c + 1) * width]))
        k_ref[:, sl] = k.astype(BF16)
        v = _dot(h, w_ref[:, 2 * d + c * width:2 * d + (c + 1) * width])
        vt_ref[sl, :] = v.T.astype(BF16)


def _flash_kernel(qt_ref, k_ref, vt_ref, lam_ref, sg_ref, o_ref, acc_ref, s0_ref, s1_ref,
                  *, lambda_init, head_dim, tq):
    seq = k_ref.shape[0]
    tk = s0_ref.shape[0]
    n_pairs = seq // (2 * tk)
    n_q = seq // tq
    last = seq - 2 * tk

    lam = lam_ref[...]
    lam_full = (jnp.exp(jnp.sum(lam[0:1] * lam[1:2], axis=-1, keepdims=True))
                - jnp.exp(jnp.sum(lam[2:3] * lam[3:4], axis=-1, keepdims=True)) + lambda_init)
    gain = jnp.concatenate([sg_ref[...]] * (tq // LANES), axis=1) * (1.0 - lambda_init)

    def aligned(start, size):
        return start if isinstance(start, int) else pl.multiple_of(start, size)

    def masked_queries(i):
        qt = qt_ref[:, pl.ds(aligned(i * tq, tq), tq)]
        row = lax.broadcasted_iota(jnp.int32, qt.shape, 0)
        zero = jnp.zeros_like(qt)
        return jnp.concatenate([jnp.where(row < head_dim, qt, zero),
                                jnp.where(row >= head_dim, qt, zero)], axis=1)

    def scores(start, s_ref, qq):
        st = _dot(k_ref[pl.ds(aligned(start, tk), tk), :], qq)
        s_ref[...] = st
        return jnp.max(st, axis=0, keepdims=True)

    def accumulate(start, s_ref, m_old, l_old, m_tile):
        m_new = jnp.maximum(m_old, m_tile)
        alpha = jnp.exp2(m_old - m_new)
        pt = jnp.exp2(s_ref[...] - m_new)
        l_new = alpha * l_old + jnp.sum(pt, axis=0, keepdims=True)
        acc_ref[...] = alpha * acc_ref[...] + _dot(vt_ref[:, pl.ds(aligned(start, tk), tk)],
                                                   pt.astype(BF16))
        return m_new, l_new

    def query_tile(i, carry):
        m_even, qq = carry

        def pair(p, c):
            m, l, m_even = c
            base = p * (2 * tk)
            m_odd = scores(base + tk, s1_ref, qq)
            m, l = accumulate(base, s0_ref, m, l, m_even)
            m_even = scores(base + 2 * tk, s0_ref, qq)
            m, l = accumulate(base + tk, s1_ref, m, l, m_odd)
            return m, l, m_even

        acc_ref[...] = jnp.zeros(acc_ref.shape, F32)
        m0 = jnp.full((1, 2 * tq), -1e30, F32)
        l0 = jnp.zeros((1, 2 * tq), F32)
        m, l, m_even = lax.fori_loop(0, n_pairs - 1, pair, (m0, l0, m_even),
                                     unroll=FLASH_PAIR_UNROLL)
        m_odd = scores(last + tk, s1_ref, qq)
        m, l = accumulate(last, s0_ref, m, l, m_even)
        qq_next = masked_queries(jnp.minimum(i + 1, n_q - 1))
        m_even_next = scores(0, s0_ref, qq_next)
        _, l_fin = accumulate(last + tk, s1_ref, m, l, m_odd)
        a = acc_ref[...] / l_fin
        o = a[:, :tq] - lam_full * a[:, tq:]
        ms = jnp.mean(o * o, axis=0, keepdims=True)
        o = o * lax.rsqrt(ms + EPS) * gain
        o_ref[pl.ds(aligned(i * tq, tq), tq), :] = o.T.astype(BF16)
        return m_even_next, qq_next

    qq0 = masked_queries(0)
    lax.fori_loop(0, n_q, query_tile, (scores(0, s0_ref, qq0), qq0))


def _proj_kernel(a_ref, x_ref, mod_ref, w_ref, o_ref):
    o_ref[...] = x_ref[...] + mod_ref[5:6, :] * _dot(a_ref[...], w_ref[...])


def _da_mixer(x, mod, layer, boff, bsz, seq, ng, w_qkv, lam, subln_g, w_out, lambda_init):
    t, d = x.shape
    head_w = d // DA_HEADS
    head_dim = head_w // 2
    tm = min(ROW_TILE, seq)
    tps = seq // tm
    cos_t, sin_p, sin_n = _rope_tables(seq, head_dim)
    tab_spec = pl.BlockSpec((tm, head_w), lambda i: (i % tps, 0))
    seq_t_spec = pl.BlockSpec((None, d, tm), lambda i: (i // tps, 0, i % tps))
    qt, k, vt = pl.pallas_call(
        functools.partial(_qkv_kernel, q_scale=head_dim ** -0.5 * math.log2(math.e),
                          rot_half=head_dim // 8),
        grid=(t // tm,),
        in_specs=[
            pl.BlockSpec((tm, d), lambda i: (i, 0)),
            _mod_spec(layer, boff, tps, d),
            _const_spec((1, d)),
            _const_spec((d, 3 * d)),
            tab_spec, tab_spec, tab_spec,
        ],
        out_specs=[seq_t_spec, pl.BlockSpec((tm, d), lambda i: (i, 0)), seq_t_spec],
        out_shape=[jax.ShapeDtypeStruct((bsz, d, seq), BF16),
                   jax.ShapeDtypeStruct((t, d), BF16),
                   jax.ShapeDtypeStruct((bsz, d, seq), BF16)],
        compiler_params=_cparams(1),
        name="da_qkv",
    )(x, mod, ng, w_qkv, cos_t, sin_p, sin_n)

    tq = min(Q_TILE, seq)
    nq = seq // tq
    tk = min(KV_TILE, seq // 2)
    assert seq % (2 * tk) == 0 and seq % tq == 0
    gain_tab = jnp.broadcast_to(subln_g[:, None], (head_w, LANES))
    attn = pl.pallas_call(
        functools.partial(_flash_kernel, lambda_init=lambda_init, head_dim=head_dim, tq=tq),
        grid=(bsz, DA_HEADS),
        in_specs=[
            pl.BlockSpec((None, head_w, seq), lambda b, hh: (b, hh, 0)),
            pl.BlockSpec((seq, head_w), lambda b, hh: (b, hh)),
            pl.BlockSpec((None, head_w, seq), lambda b, hh: (b, hh, 0)),
            _const_spec(lam.shape),
            _const_spec((head_w, LANES)),
        ],
        out_specs=pl.BlockSpec((seq, head_w), lambda b, hh: (b, hh)),
        out_shape=jax.ShapeDtypeStruct((t, d), BF16),
        scratch_shapes=[pltpu.VMEM((head_w, 2 * tq), F32),
                        pltpu.VMEM((tk, 2 * tq), F32), pltpu.VMEM((tk, 2 * tq), F32)],
        compiler_params=_cparams(2),
        name="da_flash",
    )(qt, k, vt, lam, gain_tab)

    return pl.pallas_call(
        _proj_kernel,
        grid=(t // tm,),
        in_specs=[
            pl.BlockSpec((tm, d), lambda i: (i, 0)),
            pl.BlockSpec((tm, d), lambda i: (i, 0)),
            _mod_spec(layer, boff, tps, d),
            _const_spec((d, d)),
        ],
        out_specs=pl.BlockSpec((tm, d), lambda i: (i, 0)),
        out_shape=jax.ShapeDtypeStruct((t, d), F32),
        compiler_params=_cparams(1),
        name="da_out",
    )(attn, x, mod, w_out)


def _conv_kernel(xp_ref, x_ref, xn_ref, mod_ref, ng_ref, win_ref, ck_ref, wout_ref, o_ref, g_ref,
                 *, tiles_per_seq):
    i = pl.program_id(0)
    x = x_ref[...]
    tm, d = x.shape
    halo = xp_ref.shape[0]
    ng = ng_ref[...]
    xe = jnp.concatenate([xp_ref[...], x, xn_ref[...]], axis=0)
    h = _norm_mod(xe, ng, mod_ref, 3).astype(BF16)
    cg = _dot(h, win_ref[:, d:2 * d]) * _dot(h, win_ref[:, 2 * d:])
    row = lax.broadcasted_iota(jnp.int32, cg.shape, 0)
    first = (i % tiles_per_seq) == 0
    last = (i % tiles_per_seq) == tiles_per_seq - 1
    outside = jnp.logical_or(jnp.logical_and(first, row < halo),
                             jnp.logical_and(last, row >= halo + tm))
    g_ref[...] = jnp.where(outside, 0.0, cg)
    conv = (g_ref[halo - 1:halo - 1 + tm, :] * ck_ref[0:1, :]
            + g_ref[halo:halo + tm, :] * ck_ref[1:2, :]
            + g_ref[halo + 1:halo + 1 + tm, :] * ck_ref[2:3, :])
    bg = _dot(h[halo:halo + tm, :], win_ref[:, :d])
    y = _dot((bg * conv).astype(BF16), wout_ref[...])
    o_ref[...] = x + mod_ref[5:6, :] * y


def _conv_mixer(x, mod, layer, boff, seq, ng, w_in, conv_k, w_out):
    t, d = x.shape
    tm = min(ROW_TILE, seq)
    tps = seq // tm
    halo = BF16_ROWS
    hb = tm // halo
    n_halo_blocks = t // halo
    return pl.pallas_call(
        functools.partial(_conv_kernel, tiles_per_seq=tps),
        grid=(t // tm,),
        in_specs=[
            pl.BlockSpec((halo, d), lambda i: (jnp.maximum(i * hb - 1, 0), 0)),
            pl.BlockSpec((tm, d), lambda i: (i, 0)),
            pl.BlockSpec((halo, d), lambda i: (jnp.minimum((i + 1) * hb, n_halo_blocks - 1), 0)),
            _mod_spec(layer, boff, tps, d),
            _const_spec((1, d)),
            _const_spec((d, 3 * d)),
            _const_spec(conv_k.shape),
            _const_spec((d, d)),
        ],
        out_specs=pl.BlockSpec((tm, d), lambda i: (i, 0)),
        out_shape=jax.ShapeDtypeStruct((t, d), F32),
        scratch_shapes=[pltpu.VMEM((tm + 2 * halo, d), F32)],
        compiler_params=_cparams(1),
        name="conv_mixer",
    )(x, x, x, mod, ng, w_in, conv_k, w_out)


def _trunk(x, mod, boff, p):
    bsz, seq, d = x.shape
    depth = p["norm_g"].shape[0]
    x = x.reshape(bsz * seq, d)
    for i in range(depth):
        ng = p["norm_g"][i]
        x = _ffn(x, mod, i, boff, seq, ng[0:1], p["ffn_w_gate"][i, 0], p["ffn_w_up"][i, 0],
                 p["ffn_w_down"][i, 0], 0)
        j = i // N_MIXERS
        kind = i % N_MIXERS
        if kind == 0:
            x = _sg_mixer(x, mod, i, boff, seq, ng[1:2], p["sg_w_in"][j], p["sg_b_in"][j:j + 1],
                          p["sg_ln_g"][j:j + 1], p["sg_ln_b"][j:j + 1], p["sg_w_s"][j],
                          p["sg_bs_tab"][j], p["sg_w_out"][j])
        elif kind == 1:
            lambda_init = 0.8 - 0.6 * math.exp(-0.3 * i)
            x = _da_mixer(x, mod, i, boff, bsz, seq, ng[1:2], p["da_w_qkv"][j], p["da_lambda"][j],
                          p["da_subln_g"][j], p["da_w_out"][j], lambda_init)
        else:
            x = _conv_mixer(x, mod, i, boff, seq, ng[1:2], p["conv_w_in"][j], p["conv_kernel"][j],
                            p["conv_w_out"][j])
        final_g = p["final_norm_g"].reshape(1, d) if i == depth - 1 else None
        x = _ffn(x, mod, i, boff, seq, ng[2:3], p["ffn_w_gate"][i, 1], p["ffn_w_up"][i, 1],
                 p["ffn_w_down"][i, 1], 6, final_g)
    return x.reshape(bsz, seq, d)


def kernel(x_prompt, x_sample, c_prompt, c_sample, norm_g, ada_w, ada_b, ffn_w_gate, ffn_w_up, ffn_w_down, sg_w_in, sg_b_in, sg_ln_g, sg_ln_b, sg_w_s, sg_b_s, sg_w_out, da_w_qkv, da_lambda, da_subln_g, da_w_out, conv_w_in, conv_kernel, conv_w_out, final_norm_g):
    bf = lambda w: w.astype(BF16)
    p = dict(
        norm_g=norm_g, final_norm_g=final_norm_g,
        ffn_w_gate=bf(ffn_w_gate), ffn_w_up=bf(ffn_w_up), ffn_w_down=bf(ffn_w_down),
        sg_w_in=bf(sg_w_in), sg_b_in=sg_b_in, sg_ln_g=sg_ln_g, sg_ln_b=sg_ln_b, sg_w_s=bf(sg_w_s),
        sg_bs_tab=jnp.broadcast_to(sg_b_s[..., None], sg_b_s.shape + (LANES,)),
        sg_w_out=bf(sg_w_out),
        da_w_qkv=bf(da_w_qkv), da_lambda=da_lambda, da_subln_g=da_subln_g, da_w_out=bf(da_w_out),
        conv_w_in=bf(conv_w_in), conv_kernel=conv_kernel, conv_w_out=bf(conv_w_out),
    )
    mod = _ada_mod(jnp.concatenate([c_prompt, c_sample], axis=0), ada_w, ada_b)
    y_prompt = _trunk(x_prompt, mod, 0, p)
    y_sample = _trunk(x_sample, mod, c_prompt.shape[0], p)
    return (y_prompt, y_sample)
```
